```python
import math
import jax, jax.numpy as jnp
from jax import lax
import numpy as np

D_MODEL = 1024
BATCH = 8
SEQ = 2048
DEPTH = 2
DEC_BATCH = 128
DEC_SEQ = 1
PAST_LEN = 16384
PAGE_SIZE = 128

N_EVEN = (DEPTH + 1) // 2
N_ODD = DEPTH // 2
RMS_EPS = 1e-6
NORM_EPS = 1e-5

RET_HEADS = 4
RET_DK = D_MODEL // RET_HEADS // 2
RET_DV = D_MODEL // RET_HEADS
RET_QK_WIDTH = RET_HEADS * RET_DK
RET_V_WIDTH = RET_HEADS * RET_DV
RET_CHUNK = 128
ROPE_BASE = 10000.0

S5_GROUP = 16
S5_WIDTH = D_MODEL
S5_GROUPS = S5_WIDTH // S5_GROUP
S5_STATE = 64
S5_DT_MIN = 1e-3
S5_DT_MAX = 1e-1

HG_HEAD_DIM = 128
HG_WIDTH = D_MODEL
HG_HEADS = HG_WIDTH // HG_HEAD_DIM
HG_CHUNK = 64

MIX0_WIDTH = RET_V_WIDTH + S5_WIDTH
IN0_WIDTH = 2 * RET_QK_WIDTH + 2 * RET_V_WIDTH + 2 * S5_WIDTH
SPLIT0 = (RET_QK_WIDTH, 2 * RET_QK_WIDTH, 2 * RET_QK_WIDTH + RET_V_WIDTH,
          2 * RET_QK_WIDTH + 2 * RET_V_WIDTH, 2 * RET_QK_WIDTH + 2 * RET_V_WIDTH + S5_WIDTH)
IN1_WIDTH = 4 * HG_WIDTH

kernel_name = "retnet_s5_hgrn2_hybrid_step"


def rms_norm(x, w):
    x32 = x.astype(jnp.float32)
    y = x32 * lax.rsqrt(jnp.mean(x32 * x32, axis=-1, keepdims=True) + RMS_EPS)
    return y.astype(x.dtype) * w


def head_group_norm(o, w):
    mu = jnp.mean(o, axis=-1, keepdims=True)
    c = o - mu
    y = c * lax.rsqrt(jnp.mean(c * c, axis=-1, keepdims=True) + NORM_EPS)
    return y.reshape(o.shape[0], o.shape[1], -1) * w.astype(jnp.float32)


def head_rms_norm(o, w):
    y = o * lax.rsqrt(jnp.mean(o * o, axis=-1, keepdims=True) + RMS_EPS)
    return y.reshape(o.shape[0], o.shape[1], -1) * w.astype(jnp.float32)


def rotary(x, pos):
    half = x.shape[-1] // 2
    inv = ROPE_BASE ** (-jnp.arange(half, dtype=jnp.float32) / half)
    ang = pos.astype(jnp.float32)[:, None] * inv[None, :]
    cos = jnp.cos(ang)[None, :, None, :]
    sin = jnp.sin(ang)[None, :, None, :]
    x1, x2 = x[..., :half], x[..., half:]
    return jnp.concatenate([x1 * cos - x2 * sin, x1 * sin + x2 * cos], axis=-1)


def retention(q, k, v, s0):
    bsz, t, nh, _ = q.shape
    dv = v.shape[-1]
    blk = math.gcd(t, RET_CHUNK)
    nblk = t // blk
    log_gamma = jnp.log1p(-jnp.exp2(-5.0 - jnp.arange(nh, dtype=jnp.float32)))
    idx = jnp.arange(blk, dtype=jnp.float32)
    diff = idx[:, None] - idx[None, :]
    decay_mask = jnp.where(diff >= 0, jnp.exp(log_gamma[:, None, None] * jnp.maximum(diff, 0.0)), 0.0)
    q_decay = jnp.exp(log_gamma[:, None] * (idx + 1.0))[:, :, None]
    k_decay = jnp.exp(log_gamma[:, None] * (blk - 1.0 - idx))[:, :, None]
    blk_decay = jnp.exp(log_gamma * blk)[:, None, None]

    def to_blocks(a):
        return a.reshape(bsz, nblk, blk, nh, a.shape[-1]).transpose(1, 0, 3, 2, 4)

    def step(s, inp):
        qb, kb, vb = inp
        scores = jnp.einsum('bhld,bhmd->bhlm', qb, kb) * decay_mask
        o = (jnp.einsum('bhlm,bhme->bhle', scores, vb)
             + jnp.einsum('bhld,bhde->bhle', qb * q_decay, s))
        s = blk_decay * s + jnp.einsum('bhld,bhle->bhde', kb * k_decay, vb)
        return s, o

    s, o = lax.scan(step, s0, (to_blocks(q), to_blocks(k), to_blocks(v)))
    o = o.transpose(1, 0, 3, 2, 4).reshape(bsz, t, nh, dv)
    return o, s


def s5_scan(u, lam_re, lam_im, log_dt, b_re, b_im, c_re, c_im, d_skip, s0_re, s0_im):
    bsz, t, w = u.shape
    f32 = jnp.float32
    lam_re = lam_re.astype(f32); lam_im = lam_im.astype(f32)
    b_re = b_re.astype(f32); b_im = b_im.astype(f32)
    c_re = c_re.astype(f32); c_im = c_im.astype(f32)
    ug = u.reshape(bsz, t, S5_GROUPS, S5_GROUP).transpose(1, 0, 2, 3)
    dt = jnp.exp(log_dt.astype(f32))[:, None]
    mag = jnp.exp(lam_re * dt)
    ang = lam_im * dt
    lb_re = mag * jnp.cos(ang)
    lb_im = mag * jnp.sin(ang)
    nr = lb_re - 1.0
    den = lam_re * lam_re + lam_im * lam_im
    fac_re = (nr * lam_re + lb_im * lam_im) / den
    fac_im = (lb_im * lam_re - nr * lam_im) / den
    bb_re = fac_re[..., None] * b_re - fac_im[..., None] * b_im
    bb_im = fac_re[..., None] * b_im + fac_im[..., None] * b_re
    x_re = jnp.einsum('tbgc,gpc->tbgp', ug, bb_re)
    x_im = jnp.einsum('tbgc,gpc->tbgp', ug, bb_im)
    s0_re = s0_re.astype(f32); s0_im = s0_im.astype(f32)
    x_re = x_re.at[0].add(lb_re * s0_re - lb_im * s0_im)
    x_im = x_im.at[0].add(lb_re * s0_im + lb_im * s0_re)
    a_re = jnp.broadcast_to(lb_re, (t, 1, S5_GROUPS, S5_STATE))
    a_im = jnp.broadcast_to(lb_im, (t, 1, S5_GROUPS, S5_STATE))

    def combine(e1, e2):
        a1r, a1i, b1r, b1i = e1
        a2r, a2i, b2r, b2i = e2
        return (a2r * a1r - a2i * a1i, a2r * a1i + a2i * a1r,
                a2r * b1r - a2i * b1i + b2r, a2r * b1i + a2i * b1r + b2i)

    _, _, h_re, h_im = lax.associative_scan(combine, (a_re, a_im, x_re, x_im), axis=0)
    y = jnp.einsum('tbgp,gcp->tbgc', h_re, c_re) - jnp.einsum('tbgp,gcp->tbgc', h_im, c_im)
    y = y.transpose(1, 0, 2, 3).reshape(bsz, t, w) + d_skip.astype(f32) * u
    return y, h_re[-1], h_im[-1]


def hgrn2_chunkwise(q, log_f, k, i, s0):
    bsz, t, nh, dk = q.shape
    blk = math.gcd(t, HG_CHUNK)
    nblk = t // blk
    causal = jnp.tril(jnp.ones((blk, blk), dtype=bool))[:, :, None]

    def to_blocks(a):
        return a.reshape(bsz, nblk, blk, nh, a.shape[-1]).transpose(1, 0, 3, 2, 4)

    def step(s, inp):
        qb, fb, kb, ib = inp
        cum = jnp.cumsum(fb, axis=2)
        rel = cum[:, :, :, None, :] - cum[:, :, None, :, :]
        wdec = jnp.exp(jnp.where(causal, rel, -jnp.inf))
        att = jnp.einsum('bhld,bhlmd,bhmd->bhlm', qb, wdec, kb)
        o = (jnp.einsum('bhlm,bhme->bhle', att, ib)
             + jnp.einsum('bhld,bhde->bhle', qb * jnp.exp(cum), s))
        last = cum[:, :, -1:, :]
        s = (jnp.exp(last[:, :, 0, :])[..., None] * s
             + jnp.einsum('bhld,bhle->bhde', kb * jnp.exp(last - cum), ib))
        return s, o

    s, o = lax.scan(step, s0, (to_blocks(q), to_blocks(log_f), to_blocks(k), to_blocks(i)))
    o = o.transpose(1, 0, 3, 2, 4).reshape(bsz, t, nh, dk)
    return o, s


def hgrn_lower_bound(p, layer):
    sm = jax.nn.softmax(p.astype(jnp.float32), axis=0)
    return (jnp.cumsum(sm, axis=0) - sm[0])[layer]


def even_mixer(h, pos, s_ret, s5_re0, s5_im0, w_in, gn_w, lam_re, lam_im, log_dt,
               b_re, b_im, c_re, c_im, d_skip, glu_w, glu_b, w_out):
    bsz, t, _ = h.shape
    f32 = jnp.float32
    z = h @ w_in
    q, k, v, g_a, u, g_b = jnp.split(z, SPLIT0, axis=-1)
    q = rotary(q.astype(f32).reshape(bsz, t, RET_HEADS, RET_DK), pos)
    k = rotary(k.astype(f32).reshape(bsz, t, RET_HEADS, RET_DK), pos) * (RET_DK ** -0.5)
    v = v.astype(f32).reshape(bsz, t, RET_HEADS, RET_DV)
    o_ret, s_ret_new = retention(q, k, v, s_ret.astype(f32))
    a_out = head_group_norm(o_ret, gn_w).astype(h.dtype) * jax.nn.silu(g_a)
    y5, s5_re_new, s5_im_new = s5_scan(u.astype(f32), lam_re, lam_im, log_dt, b_re, b_im,
                                       c_re, c_im, d_skip, s5_re0, s5_im0)
    y5 = jax.nn.gelu(y5).astype(h.dtype)
    y5 = y5 * jax.nn.sigmoid(y5 @ glu_w + glu_b)
    b_out = y5 * jax.nn.silu(g_b)
    out = jnp.concatenate([a_out, b_out], axis=-1) @ w_out
    return out, s_ret_new, s5_re_new, s5_im_new


def odd_mixer(h, s_hg, lower_bound, w_in, norm_w, w_out):
    bsz, t, _ = h.shape
    f32 = jnp.float32
    z = h @ w_in
    q, f_raw, i, g = jnp.split(z, 4, axis=-1)
    q = jax.nn.silu(q.astype(f32))
    f_gate = lower_bound + (1.0 - lower_bound) * jax.nn.sigmoid(f_raw.astype(f32))
    log_f = jnp.log(f_gate)
    k = 1.0 - f_gate
    shp = (bsz, t, HG_HEADS, HG_HEAD_DIM)
    o, s_new = hgrn2_chunkwise(q.reshape(shp), log_f.reshape(shp), k.reshape(shp),
                               i.astype(f32).reshape(shp), s_hg.astype(f32))
    o = head_rms_norm(o, norm_w).astype(h.dtype) * jax.nn.silu(g)
    return o @ w_out, s_new


def trunk(x, pos, ret_s, s5r_s, s5i_s, hg_s, weights):
    (norm_w, final_norm_w, w_in0, ret_gn_w, s5_lam_re, s5_lam_im, s5_log_dt, s5_b_re, s5_b_im,
     s5_c_re, s5_c_im, s5_d, s5_glu_w, s5_glu_b, w_out0, w_in1, hg_lower_bounds, hg_norm_w, w_out1) = weights
    new_ret, new_s5r, new_s5i, new_hg = [], [], [], []
    for layer in range(DEPTH):
        h = rms_norm(x, norm_w[layer])
        if layer % 2 == 0:
            e = layer // 2
            mix, sr, s5r, s5i = even_mixer(h, pos, ret_s[e], s5r_s[e], s5i_s[e], w_in0[e], ret_gn_w[e],
                                           s5_lam_re[e], s5_lam_im[e], s5_log_dt[e], s5_b_re[e], s5_b_im[e],
                                           s5_c_re[e], s5_c_im[e], s5_d[e], s5_glu_w[e], s5_glu_b[e], w_out0[e])
            new_ret.append(sr.astype(x.dtype))
            new_s5r.append(s5r.astype(x.dtype))
            new_s5i.append(s5i.astype(x.dtype))
        else:
            o_idx = layer // 2
            lb = hgrn_lower_bound(hg_lower_bounds, layer)
            mix, sh = odd_mixer(h, hg_s[o_idx], lb, w_in1[o_idx], hg_norm_w[o_idx], w_out1[o_idx])
            new_hg.append(sh.astype(x.dtype))
        x = x + mix
    y = rms_norm(x, final_norm_w)
    return y, jnp.stack(new_ret), jnp.stack(new_s5r), jnp.stack(new_s5i), jnp.stack(new_hg)


def setup_inputs(seed: int = 0) -> dict:
    key = jax.random.key(seed)
    ks = jax.random.split(key, 32)
    f32 = jnp.float32

    def nrm(k, shape, scale):
        return jax.random.normal(k, shape, f32) * scale

    n_idx = jnp.arange(S5_STATE, dtype=f32)
    return {
        'x_prompt': nrm(ks[0], (BATCH, SEQ, D_MODEL), 1.0),
        'x_sample': nrm(ks[1], (DEC_BATCH, DEC_SEQ, D_MODEL), 1.0),
        'state_ret': nrm(ks[2], (N_EVEN, DEC_BATCH, RET_HEADS, RET_DK, RET_DV), 0.5),
        'state_s5_re': nrm(ks[3], (N_EVEN, DEC_BATCH, S5_GROUPS, S5_STATE), 0.5),
        'state_s5_im': nrm(ks[4], (N_EVEN, DEC_BATCH, S5_GROUPS, S5_STATE), 0.5),
        'state_hgrn': nrm(ks[5], (N_ODD, DEC_BATCH, HG_HEADS, HG_HEAD_DIM, HG_HEAD_DIM), 0.5),
        'norm_w': 1.0 + nrm(ks[6], (DEPTH, D_MODEL), 0.02),
        'final_norm_w': 1.0 + nrm(ks[7], (D_MODEL,), 0.02),
        'w_in0': nrm(ks[8], (N_EVEN, D_MODEL, IN0_WIDTH), D_MODEL ** -0.5),
        'ret_gn_w': 1.0 + nrm(ks[9], (N_EVEN, RET_V_WIDTH), 0.02),
        's5_lam_re': -0.5 + nrm(ks[10], (N_EVEN, S5_GROUPS, S5_STATE), 0.01),
        's5_lam_im': math.pi * n_idx + nrm(ks[11], (N_EVEN, S5_GROUPS, S5_STATE), 0.01),
        's5_log_dt': jax.random.uniform(ks[12], (N_EVEN, S5_GROUPS), f32,
                                        math.log(S5_DT_MIN), math.log(S5_DT_MAX)),
        's5_b_re': nrm(ks[13], (N_EVEN, S5_GROUPS, S5_STATE, S5_GROUP), (2.0 * S5_GROUP) ** -0.5),
        's5_b_im': nrm(ks[14], (N_EVEN, S5_GROUPS, S5_STATE, S5_GROUP), (2.0 * S5_GROUP) ** -0.5),
        's5_c_re': nrm(ks[15], (N_EVEN, S5_GROUPS, S5_GROUP, S5_STATE), (2.0 * S5_STATE) ** -0.5),
        's5_c_im': nrm(ks[16], (N_EVEN, S5_GROUPS, S5_GROUP, S5_STATE), (2.0 * S5_STATE) ** -0.5),
        's5_d': nrm(ks[17], (N_EVEN, S5_WIDTH), 1.0),
        's5_glu_w': nrm(ks[18], (N_EVEN, S5_WIDTH, S5_WIDTH), S5_WIDTH ** -0.5),
        's5_glu_b': nrm(ks[19], (N_EVEN, S5_WIDTH), 0.02),
        'w_out0': nrm(ks[20], (N_EVEN, MIX0_WIDTH, D_MODEL), MIX0_WIDTH ** -0.5),
        'w_in1': nrm(ks[21], (N_ODD, D_MODEL, IN1_WIDTH), D_MODEL ** -0.5),
        'hg_lower_bounds': nrm(ks[22], (DEPTH, HG_WIDTH), 0.1),
        'hg_norm_w': 1.0 + nrm(ks[23], (N_ODD, HG_WIDTH), 0.02),
        'w_out1': nrm(ks[24], (N_ODD, HG_WIDTH, D_MODEL), HG_WIDTH ** -0.5),
    }


def reference(x_prompt, x_sample, state_ret, state_s5_re, state_s5_im, state_hgrn,
              norm_w, final_norm_w, w_in0, ret_gn_w, s5_lam_re, s5_lam_im, s5_log_dt,
              s5_b_re, s5_b_im, s5_c_re, s5_c_im, s5_d, s5_glu_w, s5_glu_b, w_out0,
              w_in1, hg_lower_bounds, hg_norm_w, w_out1):
    weights = (norm_w, final_norm_w, w_in0, ret_gn_w, s5_lam_re, s5_lam_im, s5_log_dt,
               s5_b_re, s5_b_im, s5_c_re, s5_c_im, s5_d, s5_glu_w, s5_glu_b, w_out0,
               w_in1, hg_lower_bounds, hg_norm_w, w_out1)
    bp, tp, _ = x_prompt.shape
    dt = x_prompt.dtype
    zero_ret = jnp.zeros((N_EVEN, bp, RET_HEADS, RET_DK, RET_DV), dt)
    zero_s5 = jnp.zeros((N_EVEN, bp, S5_GROUPS, S5_STATE), dt)
    zero_hg = jnp.zeros((N_ODD, bp, HG_HEADS, HG_HEAD_DIM, HG_HEAD_DIM), dt)
    pos_prompt = jnp.arange(tp, dtype=jnp.int32)
    y_prompt, ret_p, s5r_p, s5i_p, hg_p = trunk(x_prompt, pos_prompt, zero_ret, zero_s5, zero_s5,
                                                zero_hg, weights)
    pos_sample = PAST_LEN + jnp.arange(x_sample.shape[1], dtype=jnp.int32)
    y_sample, ret_s, s5r_s, s5i_s, hg_s = trunk(x_sample, pos_sample, state_ret, state_s5_re,
                                                state_s5_im, state_hgrn, weights)
    return (y_prompt, y_sample, ret_p, ret_s, s5r_p, s5i_p, s5r_s, s5i_s, hg_p, hg_s)
```

```python
import functools
import math

import numpy as np
import jax
import jax.numpy as jnp
from jax import lax
from jax.experimental import pallas as pl
from jax.experimental.pallas import tpu as pltpu

F32 = jnp.float32
BF16 = jnp.bfloat16

RMS_EPS = 1e-6
NORM_EPS = 1e-5
ROPE_BASE = 10000.0
PAST_LEN = 16384

RET_HEADS = 4
S5_GROUP = 16
S5_STATE = 64
HG_HEAD_DIM = 128

SUBLANES = 8
MXU_WIDTH = 256
VMEM_LIMIT = 56 * 1024 * 1024

RET_CHUNK = 128
HG_CHUNK = 64
S5_STEPS = 32
S5_COLS = 512
SAMPLE_TILE = 8


def _cparams(sem):
    return pltpu.CompilerParams(dimension_semantics=sem, vmem_limit_bytes=VMEM_LIMIT)


def _const_spec(shape):
    nd = len(shape)
    return pl.BlockSpec(shape, lambda *_: (0,) * nd, pipeline_mode=pl.Buffered(1))


def _rms(x, w):
    return x * lax.rsqrt(jnp.mean(x * x, axis=-1, keepdims=True) + RMS_EPS) * w


def _silu(x):
    return x * jax.nn.sigmoid(x)


def _dot(a, b):
    return jnp.dot(a, b, preferred_element_type=F32)


def _dot_nt(a, b):
    return lax.dot_general(a, b, (((1,), (1,)), ((), ())), preferred_element_type=F32)


def _dot_tn(a, b):
    return lax.dot_general(a, b, (((0,), (0,)), ((), ())), preferred_element_type=F32)


def _norm_matmul_kernel(x_ref, nw_ref, w_ref, o_ref):
    h = _rms(x_ref[...], nw_ref[...]).astype(BF16)
    o_ref[...] = _dot(h, w_ref[...])


def _norm_matmul(x2d, nw, w, nb, nt, tm, time_major):
    d = x2d.shape[1]
    n = w.shape[1]
    if time_major:
        out_shape = jax.ShapeDtypeStruct((nt * tm, nb * n), F32)
        out_spec = pl.BlockSpec((tm, n), lambda b, i: (i, b))
    else:
        out_shape = jax.ShapeDtypeStruct((nb * nt * tm, n), F32)
        out_spec = pl.BlockSpec((tm, n), lambda b, i: (b * nt + i, 0))
    return pl.pallas_call(
        _norm_matmul_kernel,
        grid=(nb, nt),
        in_specs=[pl.BlockSpec((tm, d), lambda b, i: (b * nt + i, 0)),
                  _const_spec((1, d)),
                  _const_spec((d, n))],
        out_specs=out_spec,
        out_shape=out_shape,
        compiler_params=_cparams(("parallel", "parallel")),
        name="norm_matmul",
    )(x2d, nw, w)


def _out_proj2_kernel(x_ref, a_ref, b_ref, wa_ref, wb_ref, o_ref):
    o_ref[...] = (x_ref[...] + _dot(a_ref[...].astype(BF16), wa_ref[...])
                  + _dot(b_ref[...].astype(BF16), wb_ref[...]))


def _out_proj2(x2d, a, b_tm, wa, wb, nb, nt, tm):
    d = x2d.shape[1]
    ka, kb = wa.shape[0], wb.shape[0]
    return pl.pallas_call(
        _out_proj2_kernel,
        grid=(nb, nt),
        in_specs=[pl.BlockSpec((tm, d), lambda b, i: (b * nt + i, 0)),
                  pl.BlockSpec((tm, ka), lambda b, i: (b * nt + i, 0)),
                  pl.BlockSpec((tm, kb), lambda b, i: (i, b)),
                  _const_spec((ka, d)),
                  _const_spec((kb, d))],
        out_specs=pl.BlockSpec((tm, d), lambda b, i: (b * nt + i, 0)),
        out_shape=jax.ShapeDtypeStruct(x2d.shape, F32),
        compiler_params=_cparams(("parallel", "parallel")),
        name="out_proj0",
    )(x2d, a, b_tm, wa, wb)


def _out_proj_norm_kernel(x_ref, a_ref, w_ref, fw_ref, o_ref):
    o_ref[...] = _rms(x_ref[...] + _dot(a_ref[...].astype(BF16), w_ref[...]), fw_ref[...])


def _out_proj_norm(x2d, a, w, fw, tm):
    m, d = x2d.shape
    ka = w.shape[0]
    return pl.pallas_call(
        _out_proj_norm_kernel,
        grid=(m // tm,),
        in_specs=[pl.BlockSpec((tm, d), lambda i: (i, 0)),
                  pl.BlockSpec((tm, ka), lambda i: (i, 0)),
                  _const_spec((ka, d)),
                  _const_spec((1, d))],
        out_specs=pl.BlockSpec((tm, d), lambda i: (i, 0)),
        out_shape=jax.ShapeDtypeStruct(x2d.shape, F32),
        compiler_params=_cparams(("parallel",)),
        name="out_proj1_norm",
    )(x2d, a, w, fw)


def _ret_log_gamma(nh):
    return np.log1p(-np.exp2(-5.0 - np.arange(nh, dtype=np.float32))).astype(np.float32)


def _rope_tables(pos, dk):
    half = dk // 2
    inv = ROPE_BASE ** (-jnp.arange(half, dtype=F32) / half)
    ang = pos.astype(F32)[:, None] * inv[None, :]
    cos, sin = jnp.cos(ang), jnp.sin(ang)
    return jnp.concatenate([cos, cos], axis=-1), jnp.concatenate([-sin, sin], axis=-1)


def _rope(x, cosf, sinf):
    return x * cosf + pltpu.roll(x, x.shape[-1] // 2, 1) * sinf


def _group_norm(o, w):
    c = o - jnp.mean(o, axis=-1, keepdims=True)
    return c * lax.rsqrt(jnp.mean(c * c, axis=-1, keepdims=True) + NORM_EPS) * w


def _ret_prompt_kernel(q_ref, k_ref, v_ref, g_ref, cos_ref, sin_ref, mask_ref, qd_ref, kd_ref,
                       gnw_ref, o_ref, s_out_ref, s_scr, *, nh, dk, dv, blk_decay, scale):
    c = pl.program_id(1)

    @pl.when(c == 0)
    def _():
        s_scr[...] = jnp.zeros_like(s_scr)

    cosf, sinf = cos_ref[...], sin_ref[...]
    for h in range(nh):
        q = _rope(q_ref[0, :, h * dk:(h + 1) * dk], cosf, sinf)
        k = _rope(k_ref[0, :, h * dk:(h + 1) * dk], cosf, sinf) * scale
        v = v_ref[0, :, h * dv:(h + 1) * dv].astype(BF16)
        s = s_scr[h]
        scores = _dot_nt(q.astype(BF16), k.astype(BF16)) * mask_ref[h]
        o = _dot(scores.astype(BF16), v) + _dot((q * qd_ref[h]).astype(BF16), s.astype(BF16))
        s_scr[h] = blk_decay[h] * s + _dot_tn((k * kd_ref[h]).astype(BF16), v)
        y = _group_norm(o, gnw_ref[:, h * dv:(h + 1) * dv])
        o_ref[0, :, h * dv:(h + 1) * dv] = (y * _silu(g_ref[0, :, h * dv:(h + 1) * dv])).astype(BF16)

    @pl.when(c == pl.num_programs(1) - 1)
    def _():
        s_out_ref[0, 0] = s_scr[...]


def _ret_prompt(z3, gnw, nb, t, nh, dk, dv):
    L = math.gcd(t, RET_CHUNK)
    lg = _ret_log_gamma(nh)
    idx = np.arange(L, dtype=np.float32)
    diff = idx[:, None] - idx[None, :]
    mask = np.where(diff >= 0, np.exp(lg[:, None, None] * np.maximum(diff, 0.0)), 0.0).astype(np.float32)
    qd = np.broadcast_to(np.exp(lg[:, None] * (idx + 1.0))[:, :, None], (nh, L, dk)).astype(np.float32)
    kd = np.broadcast_to(np.exp(lg[:, None] * (L - 1.0 - idx))[:, :, None], (nh, L, dk)).astype(np.float32)
    blk_decay = tuple(float(x) for x in np.exp(lg * np.float32(L)).astype(np.float32))
    cosf, sinf = _rope_tables(jnp.arange(t, dtype=jnp.int32), dk)
    qw, vw = nh * dk, nh * dv
    kern = functools.partial(_ret_prompt_kernel, nh=nh, dk=dk, dv=dv, blk_decay=blk_decay,
                             scale=float(dk) ** -0.5)
    return pl.pallas_call(
        kern,
        grid=(nb, t // L),
        in_specs=[pl.BlockSpec((1, L, qw), lambda b, c: (b, c, 0)),
                  pl.BlockSpec((1, L, qw), lambda b, c: (b, c, 1)),
                  pl.BlockSpec((1, L, vw), lambda b, c: (b, c, (2 * qw) // vw)),
                  pl.BlockSpec((1, L, vw), lambda b, c: (b, c, (2 * qw) // vw + 1)),
                  pl.BlockSpec((L, dk), lambda b, c: (c, 0)),
                  pl.BlockSpec((L, dk), lambda b, c: (c, 0)),
                  _const_spec((nh, L, L)),
                  _const_spec((nh, L, dk)),
                  _const_spec((nh, L, dk)),
                  _const_spec((1, vw))],
        out_specs=[pl.BlockSpec((1, L, vw), lambda b, c: (b, c, 0)),
                   pl.BlockSpec((1, 1, nh, dk, dv), lambda b, c: (0, b, 0, 0, 0))],
        out_shape=[jax.ShapeDtypeStruct((nb, t, vw), BF16),
                   jax.ShapeDtypeStruct((1, nb, nh, dk, dv), F32)],
        scratch_shapes=[pltpu.VMEM((nh, dk, dv), F32)],
        compiler_params=_cparams(("parallel", "arbitrary")),
        name="retention_prompt",
    )(z3, z3, z3, z3, cosf, sinf, jnp.asarray(mask), jnp.asarray(qd), jnp.asarray(kd), gnw)


def _to_col(row, eye):
    n = row.shape[1]
    return jnp.sum(jnp.where(eye, jnp.broadcast_to(row, (n, n)), 0.0), axis=1, keepdims=True)


def _eye(n):
    return lax.broadcasted_iota(jnp.int32, (n, n), 0) == lax.broadcasted_iota(jnp.int32, (n, n), 1)


def _ret_sample_kernel(q_ref, k_ref, v_ref, g_ref, cos_ref, sin_ref, gnw_ref, s_ref,
                       o_ref, s_out_ref, o_scr, *, nh, dk, dv, gamma, scale, bt):
    cosf, sinf = cos_ref[...], sin_ref[...]
    eye = _eye(dk)
    for h in range(nh):
        qh = _rope(q_ref[:, h * dk:(h + 1) * dk], cosf, sinf)
        kh = _rope(k_ref[:, h * dk:(h + 1) * dk], cosf, sinf) * scale
        for b in range(bt):
            qcol = _to_col(qh[b:b + 1], eye)
            kcol = _to_col(kh[b:b + 1], eye)
            s_new = gamma[h] * s_ref[0, b, h] + kcol * v_ref[b:b + 1, h * dv:(h + 1) * dv]
            s_out_ref[0, b, h] = s_new
            o_scr[b:b + 1, h * dv:(h + 1) * dv] = jnp.sum(qcol * s_new, axis=0, keepdims=True)
    for h in range(nh):
        sl = slice(h * dv, (h + 1) * dv)
        y = _group_norm(o_scr[:, sl], gnw_ref[:, sl])
        o_ref[:, sl] = y * _silu(g_ref[:, sl])


def _ret_sample(z2, state, gnw, pos, nh, dk, dv):
    n = z2.shape[0]
    bt = SAMPLE_TILE
    gamma = tuple(float(x) for x in np.exp(_ret_log_gamma(nh)).astype(np.float32))
    cosf, sinf = _rope_tables(jnp.full((1,), pos, jnp.int32), dk)
    qw, vw = nh * dk, nh * dv
    kern = functools.partial(_ret_sample_kernel, nh=nh, dk=dk, dv=dv, gamma=gamma,
                             scale=float(dk) ** -0.5, bt=bt)
    sspec = pl.BlockSpec((1, bt, nh, dk, dv), lambda i: (0, i, 0, 0, 0))
    return pl.pallas_call(
        kern,
        grid=(n // bt,),
        in_specs=[pl.BlockSpec((bt, qw), lambda i: (i, 0)),
                  pl.BlockSpec((bt, qw), lambda i: (i, 1)),
                  pl.BlockSpec((bt, vw), lambda i: (i, (2 * qw) // vw)),
                  pl.BlockSpec((bt, vw), lambda i: (i, (2 * qw) // vw + 1)),
                  _const_spec((1, dk)), _const_spec((1, dk)), _const_spec((1, vw)),
                  sspec],
        out_specs=[pl.BlockSpec((bt, vw), lambda i: (i, 0)), sspec],
        out_shape=[jax.ShapeDtypeStruct((n, vw), F32),
                   jax.ShapeDtypeStruct(state.shape, F32)],
        scratch_shapes=[pltpu.VMEM((bt, vw), F32)],
        compiler_params=_cparams(("parallel",)),
        name="retention_sample",
    )(z2, z2, z2, z2, cosf, sinf, gnw, state)


def _s5_discretize(lam_re, lam_im, log_dt, b_re, b_im, c_re, c_im):
    g, p = lam_re.shape
    cg = b_re.shape[-1]
    dt = jnp.exp(log_dt)[:, None]
    mag = jnp.exp(lam_re * dt)
    ang = lam_im * dt
    lb_re, lb_im = mag * jnp.cos(ang), mag * jnp.sin(ang)
    nr = lb_re - 1.0
    den = lam_re * lam_re + lam_im * lam_im
    fac_re = (nr * lam_re + lb_im * lam_im) / den
    fac_im = (lb_im * lam_re - nr * lam_im) / den
    bb_re = fac_re[..., None] * b_re - fac_im[..., None] * b_im
    bb_im = fac_re[..., None] * b_im + fac_im[..., None] * b_re
    gs = MXU_WIDTH // cg
    ns = g // gs
    eye = jnp.eye(gs, dtype=F32)

    def pack_in(bb):
        bb = bb.reshape(ns, gs, p, cg)
        return jnp.einsum('sgpc,gh->sgchp', bb, eye).reshape(ns, gs * cg, gs * p).astype(BF16)

    def pack_out(cc):
        cc = cc.reshape(ns, gs, cg, p)
        return jnp.einsum('sgcp,gh->sgphc', cc, eye).reshape(ns, gs * p, gs * cg).astype(BF16)

    return (lb_re.reshape(1, g * p), lb_im.reshape(1, g * p),
            pack_in(bb_re), pack_in(bb_im), pack_out(c_re), pack_out(-c_im))


def _s5_kernel(u_ref, g_ref, s0_ref, lamr_ref, lami_ref, bbr_ref, bbi_ref, ccr_ref, cci_ref,
               d_ref, gw_ref, gb_ref, o_ref, s_out_ref, x_scr, h_scr, *, nb, steps, ns, cols):
    i = pl.program_id(0)

    @pl.when(i == 0)
    def _():
        h_scr[...] = s0_ref[...]

    u = u_ref[...]
    cw = u.shape[1] // ns
    sw = x_scr.shape[2] // ns
    for s in range(ns):
        ub = u[:, s * cw:(s + 1) * cw].astype(BF16)
        x_scr[0, :, s * sw:(s + 1) * sw] = _dot(ub, bbr_ref[s])
        x_scr[1, :, s * sw:(s + 1) * sw] = _dot(ub, bbi_ref[s])

    for cb in range(x_scr.shape[2] // cols):
        sl = slice(cb * cols, (cb + 1) * cols)
        ar = jnp.broadcast_to(lamr_ref[:, sl], (nb, cols))
        ai = jnp.broadcast_to(lami_ref[:, sl], (nb, cols))

        def body(t, carry, sl=sl, ar=ar, ai=ai):
            hr, hi = carry
            rows = pl.ds(pl.multiple_of(t * nb, nb), nb)
            nr = ar * hr - ai * hi + x_scr[0, rows, sl]
            ni = ar * hi + ai * hr + x_scr[1, rows, sl]
            x_scr[0, rows, sl] = nr
            x_scr[1, rows, sl] = ni
            return nr, ni

        hr, hi = lax.fori_loop(0, steps, body, (h_scr[0, :, sl], h_scr[1, :, sl]))
        h_scr[0, :, sl] = hr
        h_scr[1, :, sl] = hi

    ys = []
    for s in range(ns):
        hr = x_scr[0, :, s * sw:(s + 1) * sw].astype(BF16)
        hi = x_scr[1, :, s * sw:(s + 1) * sw].astype(BF16)
        ys.append(_dot(hr, ccr_ref[s]) + _dot(hi, cci_ref[s]))
    y = jnp.concatenate(ys, axis=-1) + d_ref[...] * u
    y5 = jax.nn.gelu(y)
    gate = jax.nn.sigmoid(_dot(y5.astype(BF16), gw_ref[...]) + gb_ref[...])
    o_ref[...] = (y5 * gate * _silu(g_ref[...])).astype(BF16)

    @pl.when(i == pl.num_programs(0) - 1)
    def _():
        s_out_ref[...] = h_scr[...]


def _s5(zs, s0, params, d_skip, glu_w, glu_b, nb, steps):
    lamr, lami, bbr, bbi, ccr, cci = params
    rows, w2 = zs.shape
    w = w2 // 2
    t = rows // nb
    ns = bbr.shape[0]
    gp = lamr.shape[1]
    cols = min(S5_COLS, gp) if nb <= SUBLANES else MXU_WIDTH
    kern = functools.partial(_s5_kernel, nb=nb, steps=steps, ns=ns, cols=cols)
    r = nb * steps
    return pl.pallas_call(
        kern,
        grid=(t // steps,),
        in_specs=[pl.BlockSpec((r, w), lambda i: (i, 0)),
                  pl.BlockSpec((r, w), lambda i: (i, 1)),
                  _const_spec((2, nb, gp)),
                  _const_spec((1, gp)), _const_spec((1, gp)),
                  _const_spec(bbr.shape), _const_spec(bbi.shape),
                  _const_spec(ccr.shape), _const_spec(cci.shape),
                  _const_spec((1, w)), _const_spec((w, w)), _const_spec((1, w))],
        out_specs=[pl.BlockSpec((r, w), lambda i: (i, 0)),
                   pl.BlockSpec((2, nb, gp), lambda i: (0, 0, 0))],
        out_shape=[jax.ShapeDtypeStruct((rows, w), BF16),
                   jax.ShapeDtypeStruct((2, nb, gp), F32)],
        scratch_shapes=[pltpu.VMEM((2, r, gp), F32), pltpu.VMEM((2, nb, gp), F32)],
        compiler_params=_cparams(("arbitrary",)),
        name="s5",
    )(zs, zs, s0, lamr, lami, bbr, bbi, ccr, cci, d_skip, glu_w, glu_b)


def _hg_tables(L):
    levels = []
    s = L // 2
    while s >= 1:
        levels.append(s)
        s //= 2
    nl = len(levels)
    sel = np.zeros(((2 * nl + 2) * L, L), np.float32)
    masks = np.zeros((nl + 1, L, L), np.float32)
    idx = np.arange(L)
    for j, s in enumerate(levels):
        blk, pos = idx // (2 * s), idx % (2 * s)
        mid = blk * 2 * s + s
        upper = pos >= s
        for l in range(L):
            if upper[l]:
                sel[(2 * j) * L + l, mid[l]:l + 1] = 1.0
            else:
                sel[(2 * j + 1) * L + l, l + 1:mid[l]] = 1.0
        masks[j] = ((blk[:, None] == blk[None, :]) & upper[:, None] & (~upper[None, :])).astype(np.float32)
    for l in range(L):
        sel[(2 * nl) * L + l, :l + 1] = 1.0
        sel[(2 * nl + 1) * L + l, l + 1:] = 1.0
    masks[nl] = np.eye(L, dtype=np.float32)
    return nl, sel, masks


def _hg_gates(zq, zf, lb):
    q = _silu(zq)
    f = lb + (1.0 - lb) * jax.nn.sigmoid(zf)
    return q, f


def _head_rms(o, w):
    return o * lax.rsqrt(jnp.mean(o * o, axis=-1, keepdims=True) + RMS_EPS) * w


def _hg_prompt_kernel(zq_ref, zf_ref, zi_ref, zg_ref, lb_ref, nw_ref, sel_ref, mask_ref,
                      o_ref, s_out_ref, e_scr, st_scr, *, nh, hd, nl, L):
    c = pl.program_id(1)

    @pl.when(c == 0)
    def _():
        st_scr[...] = jnp.zeros_like(st_scr)

    q, f = _hg_gates(zq_ref[0], zf_ref[0], lb_ref[...])
    k = 1.0 - f
    logf = jnp.log(f)
    p0 = logf.astype(BF16)
    r1 = logf - p0.astype(F32)
    p1 = r1.astype(BF16)
    p2 = (r1 - p1.astype(F32)).astype(BF16)
    sel = sel_ref[...]
    e_scr[...] = _dot(sel, p0) + _dot(sel, p1) + _dot(sel, p2)

    for h in range(nh):
        sl = slice(h * hd, (h + 1) * hd)
        qh, kh = q[:, sl], k[:, sl]
        ih = zi_ref[0, :, sl].astype(BF16)
        att = _dot_nt(qh.astype(BF16), kh.astype(BF16)) * mask_ref[nl]
        for j in range(nl):
            a = (qh * jnp.exp(e_scr[(2 * j) * L:(2 * j + 1) * L, sl])).astype(BF16)
            b = (kh * jnp.exp(e_scr[(2 * j + 1) * L:(2 * j + 2) * L, sl])).astype(BF16)
            att = att + _dot_nt(a, b) * mask_ref[j]
        cq = jnp.exp(e_scr[(2 * nl) * L:(2 * nl + 1) * L, sl])
        ck = jnp.exp(e_scr[(2 * nl + 1) * L:(2 * nl + 2) * L, sl])
        st = st_scr[h]
        o = _dot(att.astype(BF16), ih) + _dot_nt((qh * cq).astype(BF16), st.astype(BF16))
        st_scr[h] = cq[L - 1:L, :] * st + _dot_tn(ih, (kh * ck).astype(BF16))
        y = _head_rms(o, nw_ref[:, sl])
        o_ref[0, :, sl] = (y * _silu(zg_ref[0, :, sl])).astype(BF16)

    @pl.when(c == pl.num_programs(1) - 1)
    def _():
        for h in range(nh):
            s_out_ref[0, 0, h] = st_scr[h].T


def _hg_prompt(z3, lb, nw, nb, t, nh, hd):
    L = math.gcd(t, HG_CHUNK)
    nl, sel, masks = _hg_tables(L)
    w = nh * hd
    kern = functools.partial(_hg_prompt_kernel, nh=nh, hd=hd, nl=nl, L=L)
    zspec = lambda j: pl.BlockSpec((1, L, w), lambda b, c, j=j: (b, c, j))
    return pl.pallas_call(
        kern,
        grid=(nb, t // L),
        in_specs=[zspec(0), zspec(1), zspec(2), zspec(3),
                  _const_spec((1, w)), _const_spec((1, w)),
                  _const_spec(sel.shape), _const_spec(masks.shape)],
        out_specs=[pl.BlockSpec((1, L, w), lambda b, c: (b, c, 0)),
                   pl.BlockSpec((1, 1, nh, hd, hd), lambda b, c: (0, b, 0, 0, 0))],
        out_shape=[jax.ShapeDtypeStruct((nb, t, w), BF16),
                   jax.ShapeDtypeStruct((1, nb, nh, hd, hd), F32)],
        scratch_shapes=[pltpu.VMEM((sel.shape[0], w), F32), pltpu.VMEM((nh, hd, hd), F32)],
        compiler_params=_cparams(("parallel", "arbitrary")),
        name="hgrn2_prompt",
    )(z3, z3, z3, z3, lb, nw, jnp.asarray(sel, BF16), jnp.asarray(masks))


def _hg_sample_kernel(zq_ref, zf_ref, zi_ref, zg_ref, lb_ref, nw_ref, s_ref,
                      o_ref, s_out_ref, o_scr, *, nh, hd, bt):
    q, f = _hg_gates(zq_ref[...], zf_ref[...], lb_ref[...])
    k = 1.0 - f
    dec = jnp.exp(jnp.log(f))
    eye = _eye(hd)
    for h in range(nh):
        sl = slice(h * hd, (h + 1) * hd)
        for b in range(bt):
            row = slice(b, b + 1)
            s_new = (_to_col(dec[row, sl], eye) * s_ref[0, b, h]
                     + _to_col(k[row, sl], eye) * zi_ref[row, sl])
            s_out_ref[0, b, h] = s_new
            o_scr[row, sl] = jnp.sum(_to_col(q[row, sl], eye) * s_new, axis=0, keepdims=True)
    for h in range(nh):
        sl = slice(h * hd, (h + 1) * hd)
        y = _head_rms(o_scr[:, sl], nw_ref[:, sl])
        o_ref[:, sl] = y * _silu(zg_ref[:, sl])


def _hg_sample(z2, state, lb, nw, nh, hd):
    n = z2.shape[0]
    bt = SAMPLE_TILE
    w = nh * hd
    kern = functools.partial(_hg_sample_kernel, nh=nh, hd=hd, bt=bt)
    zspec = lambda j: pl.BlockSpec((bt, w), lambda i, j=j: (i, j))
    sspec = pl.BlockSpec((1, bt, nh, hd, hd), lambda i: (0, i, 0, 0, 0))
    return pl.pallas_call(
        kern,
        grid=(n // bt,),
        in_specs=[zspec(0), zspec(1), zspec(2), zspec(3),
                  _const_spec((1, w)), _const_spec((1, w)), sspec],
        out_specs=[pl.BlockSpec((bt, w), lambda i: (i, 0)), sspec],
        out_shape=[jax.ShapeDtypeStruct((n, w), F32),
                   jax.ShapeDtypeStruct(state.shape, F32)],
        scratch_shapes=[pltpu.VMEM((bt, w), F32)],
        compiler_params=_cparams(("parallel",)),
        name="hgrn2_sample",
    )(z2, z2, z2, z2, lb, nw, state)


def _hg_lower_bound(p, layer):
    sm = jax.nn.softmax(p.astype(F32), axis=0)
    return (jnp.cumsum(sm, axis=0) - sm[0])[layer]


def _row_tile(t):
    return math.gcd(t, 512)


def kernel(x_prompt, x_sample, state_ret, state_s5_re, state_s5_im, state_hgrn, norm_w, final_norm_w,
           w_in0, ret_gn_w, s5_lam_re, s5_lam_im, s5_log_dt, s5_b_re, s5_b_im, s5_c_re, s5_c_im, s5_d,
           s5_glu_w, s5_glu_b, w_out0, w_in1, hg_lower_bounds, hg_norm_w, w_out1):
    bp, tp, d = x_prompt.shape
    bs, ts, _ = x_sample.shape
    assert ts == 1 and norm_w.shape[0] == 2
    nh_r = RET_HEADS
    dk, dv = d // nh_r // 2, d // nh_r
    qw, vw = nh_r * dk, nh_r * dv
    nh_h, hd = d // HG_HEAD_DIM, HG_HEAD_DIM
    past_len = PAST_LEN

    w0 = w_in0[0]
    wa0 = w0[:, :2 * qw + 2 * vw].astype(BF16)
    wb0 = w0[:, 2 * qw + 2 * vw:].astype(BF16)
    wo0a = w_out0[0][:vw].astype(BF16)
    wo0b = w_out0[0][vw:].astype(BF16)
    w1 = w_in1[0].astype(BF16)
    wo1 = w_out1[0].astype(BF16)
    glu_w = s5_glu_w[0].astype(BF16)
    glu_b = s5_glu_b[0].reshape(1, -1)
    d_skip = s5_d[0].reshape(1, -1)
    gnw = ret_gn_w[0].reshape(1, -1)
    nw0, nw1 = norm_w[0].reshape(1, -1), norm_w[1].reshape(1, -1)
    fnw = final_norm_w.reshape(1, -1)
    hnw = hg_norm_w[0].reshape(1, -1)
    lb = _hg_lower_bound(hg_lower_bounds, 1).reshape(1, -1)
    s5p = _s5_discretize(s5_lam_re[0], s5_lam_im[0], s5_log_dt[0], s5_b_re[0], s5_b_im[0],
                         s5_c_re[0], s5_c_im[0])
    gp = s5p[0].shape[1]

    def trunk(x2d, nb, t, prompt, st_ret, st_s5, st_hg):
        tm = _row_tile(t)
        nt = t // tm
        za = _norm_matmul(x2d, nw0, wa0, nb, nt, tm, time_major=False)
        zs = _norm_matmul(x2d, nw0, wb0, nb, nt, tm, time_major=True)
        if prompt:
            a_out, ret_new = _ret_prompt(za.reshape(nb, t, -1), gnw, nb, t, nh_r, dk, dv)
            a_out = a_out.reshape(nb * t, vw)
            b_out, s5_new = _s5(zs.reshape(t * nb, -1), st_s5, s5p, d_skip, glu_w, glu_b, nb,
                                math.gcd(t, S5_STEPS))
            b_tm = b_out.reshape(t, nb * d)
        else:
            a_out, ret_new = _ret_sample(za, st_ret, gnw, past_len, nh_r, dk, dv)
            b_tm, s5_new = _s5(zs, st_s5, s5p, d_skip, glu_w, glu_b, t, 1)
        x1 = _out_proj2(x2d, a_out, b_tm, wo0a, wo0b, nb, nt, tm)
        z1 = _norm_matmul(x1, nw1, w1, nb, nt, tm, time_major=False)
        if prompt:
            o1, hg_new = _hg_prompt(z1.reshape(nb, t, -1), lb, hnw, nb, t, nh_h, hd)
            o1 = o1.reshape(nb * t, d)
        else:
            o1, hg_new = _hg_sample(z1, st_hg, lb, hnw, nh_h, hd)
        y = _out_proj_norm(x1, o1, wo1, fnw, tm)
        return y, ret_new, s5_new, hg_new

    zero_s5 = jnp.zeros((2, bp, gp), F32)
    y_p, ret_p, s5_p, hg_p = trunk(x_prompt.reshape(bp * tp, d), bp, tp, True, None, zero_s5, None)
    s5_s0 = jnp.stack([state_s5_re[0].reshape(bs, gp), state_s5_im[0].reshape(bs, gp)])
    y_s, ret_s, s5_s, hg_s = trunk(x_sample.reshape(bs, d), 1, bs, False, state_ret, s5_s0, state_hgrn)

    g5 = state_s5_re.shape[2]
    return (y_p.reshape(bp, tp, d), y_s.reshape(bs, 1, d), ret_p, ret_s,
            s5_p[0].reshape(1, bp, g5, -1), s5_p[1].reshape(1, bp, g5, -1),
            s5_s[0].reshape(1, bs, g5, -1), s5_s[1].reshape(1, bs, g5, -1),
            hg_p, hg_s)
```
